```python
import math
import jax
import jax.numpy as jnp
from jax import lax
import numpy as np


D_MODEL = 1024
BATCH = 2
SEQ = 8192
DEPTH = 1

GRID_W = 64
CTX_LEN = 256
EPS = 1e-6
NEG_INF = -1e30

D_A = 512
SSM_CPG = 16
SSM_GROUPS = D_A // SSM_CPG
SSM_STATE = 64
DT_MIN = 1e-3
DT_MAX = 1e-1

NA_HEADS = 8
NA_HEAD_DIM = 64
D_B = NA_HEADS * NA_HEAD_DIM
WIN_R = 8
WIN_C = 16
Q_BLOCK = 16
K_BLOCK_W = Q_BLOCK + WIN_C
N_COL_BLOCKS = GRID_W // Q_BLOCK

IN_COLS = D_A + 3 * D_B + 2 * D_MODEL
SPLIT_POINTS = (D_A, D_A + D_B, D_A + 2 * D_B, D_A + 3 * D_B, D_A + 3 * D_B + D_MODEL)

N_EXPERTS = 16
CAPACITY_FACTOR = 2
D_EXPERT = 1024

kernel_name = "hybrid_s5_natten_ec_moe_dit_block"


def rmsnorm(x, g):
    xf = x.astype(jnp.float32)
    y = xf * lax.rsqrt(jnp.mean(xf * xf, axis=-1, keepdims=True) + EPS)
    return (y * g.astype(jnp.float32)).astype(x.dtype)


def modulate(h, shift, scale):
    return h * (1 + scale[:, None, :]) + shift[:, None, :]


def s5_discretize(lam_re, lam_im, log_dt, b_re, b_im):
    lam = lax.complex(lam_re.astype(jnp.float32), lam_im.astype(jnp.float32))
    dt = jnp.exp(log_dt.astype(jnp.float32))[:, None]
    lam_bar = jnp.exp(lam * dt)
    b = lax.complex(b_re.astype(jnp.float32), b_im.astype(jnp.float32))
    b_bar = ((lam_bar - 1.0) / lam)[..., None] * b
    return lam_bar, b_bar


def s5_drive(u, b_bar):
    re = jnp.einsum('bngc,gpc->bngp', u, jnp.real(b_bar))
    im = jnp.einsum('bngc,gpc->bngp', u, jnp.imag(b_bar))
    return lax.complex(re, im)


def s5_readout(xs, c_re, c_im):
    return (jnp.einsum('bngp,gcp->bngc', jnp.real(xs), c_re)
            - jnp.einsum('bngp,gcp->bngc', jnp.imag(xs), c_im))


def _linear_recurrence(e1, e2):
    a1, b1 = e1
    a2, b2 = e2
    return a1 * a2, a2 * b1 + b2


def s5_scan(lam_bar, bu, h0, reverse):
    if h0 is not None:
        first = -1 if reverse else 0
        bu = bu.at[:, first].add(lam_bar * h0)
    a = jnp.broadcast_to(lam_bar, bu.shape)
    _, xs = lax.associative_scan(_linear_recurrence, (a, bu), axis=1, reverse=reverse)
    return xs


def s5_glu(y, w_glu, b_glu):
    a = jax.nn.gelu(y)
    return a * jax.nn.sigmoid(a @ w_glu.astype(jnp.float32) + b_glu.astype(jnp.float32))


def s5_branch(u, u_ctx, lam_re, lam_im, log_dt, b_re, b_im, c_re, c_im, d_skip, w_glu, b_glu, update_ctx):
    bsz, n_lat, _ = u.shape
    n_ctx = u_ctx.shape[1]
    u32 = u.astype(jnp.float32).reshape(bsz, n_lat, SSM_GROUPS, SSM_CPG)
    uc32 = u_ctx.astype(jnp.float32).reshape(bsz, n_ctx, SSM_GROUPS, SSM_CPG)
    d32 = d_skip.astype(jnp.float32).reshape(SSM_GROUPS, SSM_CPG)
    y = u32 * d32
    yc = uc32 * d32 if update_ctx else None
    for direction, reverse in ((0, False), (1, True)):
        lam_bar, b_bar = s5_discretize(lam_re[direction], lam_im[direction], log_dt[direction],
                                       b_re[direction], b_im[direction])
        c_r = c_re[direction].astype(jnp.float32)
        c_i = c_im[direction].astype(jnp.float32)
        xs_ctx = s5_scan(lam_bar, s5_drive(uc32, b_bar), None, reverse)
        h_end = xs_ctx[:, 0] if reverse else xs_ctx[:, -1]
        xs_lat = s5_scan(lam_bar, s5_drive(u32, b_bar), h_end, reverse)
        y = y + s5_readout(xs_lat, c_r, c_i)
        if update_ctx:
            yc = yc + s5_readout(xs_ctx, c_r, c_i)
    out = s5_glu(y.reshape(bsz, n_lat, D_A), w_glu, b_glu).astype(u.dtype)
    out_ctx = s5_glu(yc.reshape(bsz, n_ctx, D_A), w_glu, b_glu).astype(u.dtype) if update_ctx else None
    return out, out_ctx


def na_branch(q, k, v, q_ctx, k_ctx, v_ctx, rpb, update_ctx):
    bsz, n_lat, n_heads, head_dim = q.shape
    rows = n_lat // GRID_W
    win_r = min(WIN_R, rows)
    n_nb = win_r * K_BLOCK_W
    r = jnp.arange(rows)
    row_idx = jnp.clip(r - WIN_R // 2, 0, rows - win_r)[:, None] + jnp.arange(win_r)[None, :]
    blk = jnp.arange(N_COL_BLOCKS)
    col_idx = (jnp.clip(blk * Q_BLOCK - WIN_C // 2, 0, GRID_W - K_BLOCK_W)[:, None]
               + jnp.arange(K_BLOCK_W)[None, :])
    key_row = jnp.repeat(row_idx, K_BLOCK_W, axis=1)
    key_col = jnp.tile(col_idx, (1, win_r))
    tok_idx = key_row[:, None, :] * GRID_W + key_col[None, :, :]
    k_nb = k[:, tok_idx]
    v_nb = v[:, tok_idx]
    q_blk = q.reshape(bsz, rows, N_COL_BLOCKS, Q_BLOCK, n_heads, head_dim)
    s_nb = jnp.einsum('brjqhd,brjkhd->bhrjqk', q_blk, k_nb).astype(jnp.float32)
    q_col = blk[:, None] * Q_BLOCK + jnp.arange(Q_BLOCK)[None, :]
    q_cs = jnp.clip(q_col - WIN_C // 2, 0, GRID_W - WIN_C)
    in_win = ((key_col[:, None, :] >= q_cs[:, :, None])
              & (key_col[:, None, :] < q_cs[:, :, None] + WIN_C))
    d_col = jnp.clip(key_col[:, None, :] - q_col[:, :, None] + WIN_C - 1, 0, 2 * WIN_C - 2)
    d_row = key_row - r[:, None] + WIN_R - 1
    bias = rpb[:, d_row[:, None, None, :], d_col[None]].astype(jnp.float32)
    s_nb = jnp.where(in_win, s_nb + bias, NEG_INF)
    s_ctx = jnp.einsum('bqhd,bkhd->bhqk', q, k_ctx).astype(jnp.float32)
    s_ctx = s_ctx.reshape(bsz, n_heads, rows, N_COL_BLOCKS, Q_BLOCK, -1)
    p = jax.nn.softmax(jnp.concatenate([s_nb, s_ctx], axis=-1), axis=-1).astype(v.dtype)
    o = (jnp.einsum('bhrjqk,brjkhd->brjqhd', p[..., :n_nb], v_nb)
         + jnp.einsum('bhrjqk,bkhd->brjqhd', p[..., n_nb:], v_ctx))
    out = o.reshape(bsz, n_lat, n_heads * head_dim)
    out_ctx = None
    if update_ctx:
        sc = jnp.einsum('bqhd,bkhd->bhqk', q_ctx, k_ctx).astype(jnp.float32)
        pc = jax.nn.softmax(sc, axis=-1).astype(v_ctx.dtype)
        out_ctx = jnp.einsum('bhqk,bkhd->bqhd', pc, v_ctx).reshape(bsz, q_ctx.shape[1], n_heads * head_dim)
    return out, out_ctx


def token_mixer(h, hc, w_in, ssm_lam_re, ssm_lam_im, ssm_log_dt, ssm_b_re, ssm_b_im, ssm_c_re, ssm_c_im,
                ssm_d, w_glu, b_glu, q_norm, k_norm, na_rpb, w_proj_a, w_proj_b, w_out, update_ctx):
    def project(t):
        z = t @ w_in
        u, q, k, v, ga, gb = jnp.split(z, list(SPLIT_POINTS), axis=-1)
        shape = t.shape[:2] + (NA_HEADS, NA_HEAD_DIM)
        q = rmsnorm(q.reshape(shape), q_norm) * (NA_HEAD_DIM ** -0.5)
        k = rmsnorm(k.reshape(shape), k_norm)
        return u, q, k, v.reshape(shape), ga, gb

    def merge(ya, yb, ga, gb):
        return (jax.nn.sigmoid(ga) * (ya @ w_proj_a) + jax.nn.sigmoid(gb) * (yb @ w_proj_b)) @ w_out

    u, q, k, v, ga, gb = project(h)
    uc, qc, kc, vc, gac, gbc = project(hc)
    ya, yac = s5_branch(u, uc, ssm_lam_re, ssm_lam_im, ssm_log_dt, ssm_b_re, ssm_b_im, ssm_c_re, ssm_c_im,
                        ssm_d, w_glu, b_glu, update_ctx)
    yb, ybc = na_branch(q, k, v, qc, kc, vc, na_rpb, update_ctx)
    y = merge(ya, yb, ga, gb)
    yc = merge(yac, ybc, gac, gbc) if update_ctx else None
    return y, yc


def ec_moe_set(h, w_router, w_e_gate, w_e_up, w_e_down):
    n_tok = h.shape[0]
    cap = CAPACITY_FACTOR * n_tok // N_EXPERTS
    aff = jax.nn.softmax((h @ w_router).astype(jnp.float32), axis=-1)
    g, idx = lax.top_k(aff.T, cap)
    xe = h[idx]
    hid = jax.nn.silu(jnp.einsum('ecd,edf->ecf', xe, w_e_gate)) * jnp.einsum('ecd,edf->ecf', xe, w_e_up)
    ye = jnp.einsum('ecf,efd->ecd', hid, w_e_down) * g[..., None].astype(h.dtype)
    return jnp.zeros_like(h).at[idx.reshape(-1)].add(ye.reshape(-1, h.shape[-1]))


def ec_moe(h, w_router, w_e_gate, w_e_up, w_e_down):
    return jax.vmap(ec_moe_set, in_axes=(0, None, None, None, None))(h, w_router, w_e_gate, w_e_up, w_e_down)


def block(x, ctx_s, mod, mod_ctx, norm_mix, norm_ffn, w_in, ssm_lam_re, ssm_lam_im, ssm_log_dt, ssm_b_re,
          ssm_b_im, ssm_c_re, ssm_c_im, ssm_d, w_glu, b_glu, q_norm, k_norm, na_rpb, w_proj_a, w_proj_b,
          w_out, w_router, w_e_gate, w_e_up, w_e_down, update_ctx):
    sh1, sc1, g1, sh2, sc2, g2 = jnp.split(mod, 6, axis=-1)
    csh1, csc1, cg1, csh2, csc2, cg2 = jnp.split(mod_ctx, 6, axis=-1)
    h = modulate(rmsnorm(x, norm_mix), sh1, sc1)
    hc = modulate(rmsnorm(ctx_s, norm_mix), csh1, csc1)
    y, yc = token_mixer(h, hc, w_in, ssm_lam_re, ssm_lam_im, ssm_log_dt, ssm_b_re, ssm_b_im, ssm_c_re,
                        ssm_c_im, ssm_d, w_glu, b_glu, q_norm, k_norm, na_rpb, w_proj_a, w_proj_b, w_out,
                        update_ctx)
    x = x + g1[:, None, :] * y
    h2 = modulate(rmsnorm(x, norm_ffn), sh2, sc2)
    x = x + g2[:, None, :] * ec_moe(h2, w_router, w_e_gate, w_e_up, w_e_down)
    if update_ctx:
        ctx_s = ctx_s + cg1[:, None, :] * yc
        hc2 = modulate(rmsnorm(ctx_s, norm_ffn), csh2, csc2)
        ctx_s = ctx_s + cg2[:, None, :] * ec_moe(hc2, w_router, w_e_gate, w_e_up, w_e_down)
    return x, ctx_s


def setup_inputs(seed: int = 0) -> dict:
    key = jax.random.key(seed)
    ks = jax.random.split(key, 32)
    f32 = jnp.float32

    def nrm(k, shape, s):
        return jax.random.normal(k, shape, f32) * s

    two = 2
    lam_im_base = jnp.pi * jnp.arange(SSM_STATE, dtype=f32)
    return {
        'x': nrm(ks[0], (BATCH, SEQ, D_MODEL), 1.0),
        'c': nrm(ks[1], (BATCH, D_MODEL), 1.0),
        'ctx': nrm(ks[2], (BATCH, CTX_LEN, D_MODEL), 1.0),
        'c_ctx': nrm(ks[3], (D_MODEL,), 1.0),
        'w_ada': nrm(ks[4], (DEPTH, D_MODEL, 6 * D_MODEL), 0.5 * D_MODEL ** -0.5),
        'b_ada': nrm(ks[5], (DEPTH, 6 * D_MODEL), 0.01),
        'norm_mix': 1.0 + nrm(ks[6], (DEPTH, D_MODEL), 0.01),
        'norm_ffn': 1.0 + nrm(ks[7], (DEPTH, D_MODEL), 0.01),
        'w_in': nrm(ks[8], (DEPTH, D_MODEL, IN_COLS), D_MODEL ** -0.5),
        'ssm_lam_re': -0.5 + nrm(ks[9], (DEPTH, two, SSM_GROUPS, SSM_STATE), 0.01),
        'ssm_lam_im': lam_im_base + nrm(ks[10], (DEPTH, two, SSM_GROUPS, SSM_STATE), 0.01),
        'ssm_log_dt': jax.random.uniform(ks[11], (DEPTH, two, SSM_GROUPS), f32,
                                         minval=math.log(DT_MIN), maxval=math.log(DT_MAX)),
        'ssm_b_re': nrm(ks[12], (DEPTH, two, SSM_GROUPS, SSM_STATE, SSM_CPG), (2 * SSM_CPG) ** -0.5),
        'ssm_b_im': nrm(ks[13], (DEPTH, two, SSM_GROUPS, SSM_STATE, SSM_CPG), (2 * SSM_CPG) ** -0.5),
        'ssm_c_re': nrm(ks[14], (DEPTH, two, SSM_GROUPS, SSM_CPG, SSM_STATE), (2 * SSM_STATE) ** -0.5),
        'ssm_c_im': nrm(ks[15], (DEPTH, two, SSM_GROUPS, SSM_CPG, SSM_STATE), (2 * SSM_STATE) ** -0.5),
        'ssm_d': nrm(ks[16], (DEPTH, D_A), 1.0),
        'w_glu': nrm(ks[17], (DEPTH, D_A, D_A), D_A ** -0.5),
        'b_glu': nrm(ks[18], (DEPTH, D_A), 0.01),
        'q_norm': 1.0 + nrm(ks[19], (DEPTH, NA_HEAD_DIM), 0.01),
        'k_norm': 1.0 + nrm(ks[20], (DEPTH, NA_HEAD_DIM), 0.01),
        'na_rpb': nrm(ks[21], (DEPTH, NA_HEADS, 2 * WIN_R - 1, 2 * WIN_C - 1), 0.02),
        'w_proj_a': nrm(ks[22], (DEPTH, D_A, D_MODEL), D_A ** -0.5),
        'w_proj_b': nrm(ks[23], (DEPTH, D_B, D_MODEL), D_B ** -0.5),
        'w_out': nrm(ks[24], (DEPTH, D_MODEL, D_MODEL), D_MODEL ** -0.5),
        'w_router': nrm(ks[25], (DEPTH, D_MODEL, N_EXPERTS), D_MODEL ** -0.5),
        'w_e_gate': nrm(ks[26], (DEPTH, N_EXPERTS, D_MODEL, D_EXPERT), D_MODEL ** -0.5),
        'w_e_up': nrm(ks[27], (DEPTH, N_EXPERTS, D_MODEL, D_EXPERT), D_MODEL ** -0.5),
        'w_e_down': nrm(ks[28], (DEPTH, N_EXPERTS, D_EXPERT, D_MODEL), D_EXPERT ** -0.5),
    }


def reference(x, c, ctx, c_ctx, w_ada, b_ada, norm_mix, norm_ffn, w_in, ssm_lam_re, ssm_lam_im, ssm_log_dt,
              ssm_b_re, ssm_b_im, ssm_c_re, ssm_c_im, ssm_d, w_glu, b_glu, q_norm, k_norm, na_rpb, w_proj_a,
              w_proj_b, w_out, w_router, w_e_gate, w_e_up, w_e_down):
    ctx_s = ctx
    for layer in range(DEPTH):
        update_ctx = layer < DEPTH - 1
        mod = jax.nn.silu(c) @ w_ada[layer] + b_ada[layer]
        mod_ctx = jnp.broadcast_to(jax.nn.silu(c_ctx) @ w_ada[layer] + b_ada[layer], mod.shape)
        x, ctx_s = block(x, ctx_s, mod, mod_ctx, norm_mix[layer], norm_ffn[layer], w_in[layer],
                         ssm_lam_re[layer], ssm_lam_im[layer], ssm_log_dt[layer], ssm_b_re[layer],
                         ssm_b_im[layer], ssm_c_re[layer], ssm_c_im[layer], ssm_d[layer], w_glu[layer],
                         b_glu[layer], q_norm[layer], k_norm[layer], na_rpb[layer], w_proj_a[layer],
                         w_proj_b[layer], w_out[layer], w_router[layer], w_e_gate[layer], w_e_up[layer],
                         w_e_down[layer], update_ctx)
    return x
```

```python
import functools
import math

import numpy as np
import jax
import jax.numpy as jnp
from jax import lax
from jax.experimental import pallas as pl
from jax.experimental.pallas import tpu as pltpu

F32 = jnp.float32
BF16 = jnp.bfloat16

D_MODEL = 1024
D_A = 512
SSM_CPG = 16
SSM_GROUPS = D_A // SSM_CPG
SSM_STATE = 64
NA_HEADS = 8
NA_HEAD_DIM = 64
D_B = NA_HEADS * NA_HEAD_DIM
GRID_W = 64
WIN_R = 8
WIN_C = 16
N_EXPERTS = 16
CAPACITY_FACTOR = 2
D_EXPERT = 1024
EPS = 1e-6
NEG_INF = -1e30

LANES = 128
NCHUNK = LANES
ROWS_PER_QBLOCK = 4
VMEM_LIMIT = 56 * 1024 * 1024


def _cparams(*sem):
    return pltpu.CompilerParams(dimension_semantics=sem, vmem_limit_bytes=VMEM_LIMIT)


def _dot(a, b):
    return jnp.dot(a, b, preferred_element_type=F32)


def _dot_nt(a, b):
    return lax.dot_general(a, b, (((1,), (1,)), ((), ())), preferred_element_type=F32)


def _adaln_kernel(c_ref, w_ref, b_ref, o_ref):
    c = c_ref[...]
    a = c * jax.nn.sigmoid(c)
    o_ref[...] = _dot(a, w_ref[...]) + b_ref[...]


def _adaln(cc, w_ada, b_ada):
    rows, d = cc.shape
    n_out = w_ada.shape[1]
    return pl.pallas_call(
        _adaln_kernel,
        grid=(n_out // d,),
        in_specs=[pl.BlockSpec((rows, d), lambda j: (0, 0)),
                  pl.BlockSpec((d, d), lambda j: (0, j)),
                  pl.BlockSpec((1, d), lambda j: (0, j))],
        out_specs=pl.BlockSpec((rows, d), lambda j: (0, j)),
        out_shape=jax.ShapeDtypeStruct((rows, n_out), F32),
        compiler_params=_cparams("arbitrary"),
        name="adaln",
    )(cc, w_ada, b_ada.reshape(1, n_out))


def _norm_mod(xs, gain, scale, shift):
    ms = jnp.mean(xs * xs, axis=-1, keepdims=True)
    return (xs * lax.rsqrt(ms + EPS) * gain) * (1.0 + scale) + shift


def _head_rmsnorm(z, hsum, gain):
    ms = _dot((z * z).astype(BF16), hsum)
    return z * lax.rsqrt(ms + EPS) * gain


def _inproj_kernel(x_ref, sc_ref, sh_ref, gn_ref, wut_ref, wq_ref, wk_ref, wv_ref, wga_ref, wgb_ref,
                   hsum_ref, qn_ref, kn_ref,
                   ut_ref, q_ref, k_ref, v_ref, ga_ref, gb_ref, hbuf, *, slabs, nc):
    d = D_MODEL
    for s in range(slabs):
        xs = x_ref[0, :, s * d:(s + 1) * d]
        hbuf[s * nc:(s + 1) * nc, :] = _norm_mod(xs, gn_ref[...], sc_ref[0], sh_ref[0]).astype(BF16)
    hb = hbuf[...]
    ut = _dot_nt(wut_ref[...], hb)
    q = _head_rmsnorm(_dot(hb, wq_ref[...]), hsum_ref[...], qn_ref[...])
    k = _head_rmsnorm(_dot(hb, wk_ref[...]), hsum_ref[...], kn_ref[...])
    v = _dot(hb, wv_ref[...])
    ga = jax.nn.sigmoid(_dot(hb, wga_ref[...]))
    gb = jax.nn.sigmoid(_dot(hb, wgb_ref[...]))
    for s in range(slabs):
        rows = slice(s * nc, (s + 1) * nc)
        ut_ref[0, s] = ut[:, s * nc:(s + 1) * nc].astype(BF16)
        q_ref[0, :, s * D_B:(s + 1) * D_B] = q[rows].astype(BF16)
        k_ref[0, :, s * D_B:(s + 1) * D_B] = k[rows].astype(BF16)
        v_ref[0, :, s * D_B:(s + 1) * D_B] = v[rows].astype(BF16)
        ga_ref[0, :, s * d:(s + 1) * d] = ga[rows].astype(BF16)
        gb_ref[0, :, s * d:(s + 1) * d] = gb[rows].astype(BF16)


def _inproj(x, scale, shift, gain, w, hsum, qn, kn, chunk_len, slabs=4):
    bsz, n, d = x.shape
    nc = n // chunk_len
    xv = x.reshape(bsz, nc, chunk_len * d)
    const = lambda shape: pl.BlockSpec(shape, lambda b, j: (0,) * len(shape))
    outs = pl.pallas_call(
        functools.partial(_inproj_kernel, slabs=slabs, nc=nc),
        grid=(bsz, chunk_len // slabs),
        in_specs=[pl.BlockSpec((1, nc, slabs * d), lambda b, j: (b, 0, j)),
                  pl.BlockSpec((1, 1, d), lambda b, j: (b, 0, 0)),
                  pl.BlockSpec((1, 1, d), lambda b, j: (b, 0, 0)),
                  const((1, d)),
                  const((D_A, d)), const((d, D_B)), const((d, D_B)), const((d, D_B)),
                  const((d, d)), const((d, d)), const((D_B, D_B)), const((1, D_B)), const((1, D_B))],
        out_specs=[pl.BlockSpec((1, slabs, D_A, nc), lambda b, j: (b, j, 0, 0)),
                   pl.BlockSpec((1, nc, slabs * D_B), lambda b, j: (b, 0, j)),
                   pl.BlockSpec((1, nc, slabs * D_B), lambda b, j: (b, 0, j)),
                   pl.BlockSpec((1, nc, slabs * D_B), lambda b, j: (b, 0, j)),
                   pl.BlockSpec((1, nc, slabs * d), lambda b, j: (b, 0, j)),
                   pl.BlockSpec((1, nc, slabs * d), lambda b, j: (b, 0, j))],
        out_shape=[jax.ShapeDtypeStruct((bsz, chunk_len, D_A, nc), BF16),
                   jax.ShapeDtypeStruct((bsz, nc, chunk_len * D_B), BF16),
                   jax.ShapeDtypeStruct((bsz, nc, chunk_len * D_B), BF16),
                   jax.ShapeDtypeStruct((bsz, nc, chunk_len * D_B), BF16),
                   jax.ShapeDtypeStruct((bsz, nc, chunk_len * d), BF16),
                   jax.ShapeDtypeStruct((bsz, nc, chunk_len * d), BF16)],
        scratch_shapes=[pltpu.VMEM((slabs * nc, d), BF16)],
        compiler_params=_cparams("arbitrary", "arbitrary"),
        name="inproj",
    )(xv, scale, shift, gain, w["ut"], w["q"], w["k"], w["v"], w["ga"], w["gb"], hsum, qn, kn)
    ut, q, k, v, ga, gb = outs
    return (ut, q.reshape(bsz, n, D_B), k.reshape(bsz, n, D_B), v.reshape(bsz, n, D_B), ga, gb)


def _inproj_ctx_kernel(x_ref, sc_ref, sh_ref, gn_ref, wu_ref, wk_ref, wv_ref, hsum_ref, kn_ref,
                       u_ref, k_ref, v_ref):
    hb = _norm_mod(x_ref[...], gn_ref[...], sc_ref[...], sh_ref[...]).astype(BF16)
    u_ref[...] = _dot(hb, wu_ref[...])
    k_ref[...] = _head_rmsnorm(_dot(hb, wk_ref[...]), hsum_ref[...], kn_ref[...]).astype(BF16)
    v_ref[...] = _dot(hb, wv_ref[...]).astype(BF16)


def _inproj_ctx(ctx2d, scale, shift, gain, w, hsum, kn):
    m, d = ctx2d.shape
    return pl.pallas_call(
        _inproj_ctx_kernel,
        out_shape=[jax.ShapeDtypeStruct((m, D_A), F32),
                   jax.ShapeDtypeStruct((m, D_B), BF16),
                   jax.ShapeDtypeStruct((m, D_B), BF16)],
        compiler_params=pltpu.CompilerParams(vmem_limit_bytes=VMEM_LIMIT),
        name="inproj_ctx",
    )(ctx2d, scale, shift, gain, w["u"], w["k"], w["v"], hsum, kn)


def _cmul(ar, ai, xr, xi):
    return ar * xr - ai * xi, ar * xi + ai * xr


def _ssm_kernel(x_ref, xc_ref, kcat_ref, wst_ref, vst_ref, pw_ref, y_ref, t_scr, *, chunk_len, nc, nctx, bsz):
    lc = chunk_len * SSM_CPG
    p = SSM_STATE
    kc = kcat_ref[0]
    width = kc.shape[1]
    per_vreg = LANES // SSM_CPG
    for r in range(per_vreg):
        rk = kc if r == 0 else pltpu.roll(kc, width - SSM_CPG * r, axis=1)
        for a in range(chunk_len // per_vreg):
            t = chunk_len - 1 - (per_vreg * a + r)
            t_scr[t * SSM_CPG:(t + 1) * SSM_CPG, :] = rk[:, LANES * a:LANES * a + lc].astype(BF16)

    xs = [x_ref[b].reshape(lc, nc) for b in range(bsz)]
    xg = jnp.concatenate(xs, axis=1)
    xall = jnp.concatenate([xg, xc_ref[...].reshape(lc, LANES)], axis=1)
    st = _dot(wst_ref[0], xall)
    sctx = st[:, bsz * nc:]

    lane = lax.broadcasted_iota(jnp.int32, (1, nc), 1)
    pw = [[pw_ref[0, k, j] for j in range(4)] for k in range(pw_ref.shape[1])]
    alr, ali, blr, bli = pw[0]
    n_steps = int(math.log2(nc))
    hin = []
    for b in range(bsz):
        fr, fi = st[0:p, b * nc:(b + 1) * nc], st[p:2 * p, b * nc:(b + 1) * nc]
        br, bi = st[2 * p:3 * p, b * nc:(b + 1) * nc], st[3 * p:4 * p, b * nc:(b + 1) * nc]
        hfr = hfi = hbr = hbi = jnp.zeros((p, 1), F32)
        for c in range(nctx):
            col = sctx[:, b * nctx + c:b * nctx + c + 1]
            hfr, hfi = _cmul(alr, ali, hfr, hfi)
            hfr, hfi = hfr + col[0:p], hfi + col[p:2 * p]
            colb = sctx[:, b * nctx + nctx - 1 - c:b * nctx + nctx - c]
            hbr, hbi = _cmul(blr, bli, hbr, hbi)
            hbr, hbi = hbr + colb[2 * p:3 * p], hbi + colb[3 * p:4 * p]
        ifr, ifi = _cmul(alr, ali, hfr, hfi)
        ibr, ibi = _cmul(blr, bli, hbr, hbi)
        first, last = lane == 0, lane == nc - 1
        fr, fi = fr + jnp.where(first, ifr, 0.0), fi + jnp.where(first, ifi, 0.0)
        br, bi = br + jnp.where(last, ibr, 0.0), bi + jnp.where(last, ibi, 0.0)
        for k in range(n_steps):
            sh = 1 << k
            akr, aki, bkr, bki = pw[k]
            rr, ri = _cmul(akr, aki, pltpu.roll(fr, sh, axis=1), pltpu.roll(fi, sh, axis=1))
            m = lane >= sh
            fr, fi = fr + jnp.where(m, rr, 0.0), fi + jnp.where(m, ri, 0.0)
            rr, ri = _cmul(bkr, bki, pltpu.roll(br, nc - sh, axis=1), pltpu.roll(bi, nc - sh, axis=1))
            m = lane < nc - sh
            br, bi = br + jnp.where(m, rr, 0.0), bi + jnp.where(m, ri, 0.0)
        hin.append(jnp.concatenate([
            jnp.where(first, hfr, pltpu.roll(fr, 1, axis=1)),
            jnp.where(first, hfi, pltpu.roll(fi, 1, axis=1)),
            jnp.where(last, hbr, pltpu.roll(br, nc - 1, axis=1)),
            jnp.where(last, hbi, pltpu.roll(bi, nc - 1, axis=1))], axis=0))
    hall = jnp.concatenate(hin, axis=1).astype(BF16)
    y = _dot(t_scr[...], xg) + _dot(vst_ref[0], hall)
    for b in range(bsz):
        y_ref[b] = y[:, b * nc:(b + 1) * nc].reshape(chunk_len, SSM_CPG, nc).astype(BF16)


def _ssm(ut, xc, prm, chunk_len, nctx):
    bsz, _, _, nc = ut.shape
    lc = chunk_len * SSM_CPG
    g = SSM_GROUPS
    return pl.pallas_call(
        functools.partial(_ssm_kernel, chunk_len=chunk_len, nc=nc, nctx=nctx, bsz=bsz),
        grid=(g,),
        in_specs=[pl.BlockSpec((bsz, chunk_len, SSM_CPG, nc), lambda i: (0, 0, i, 0)),
                  pl.BlockSpec((chunk_len, SSM_CPG, LANES), lambda i: (0, i, 0)),
                  pl.BlockSpec((1, SSM_CPG, 2 * lc), lambda i: (i, 0, 0)),
                  pl.BlockSpec((1, 4 * SSM_STATE, lc), lambda i: (i, 0, 0)),
                  pl.BlockSpec((1, lc, 4 * SSM_STATE), lambda i: (i, 0, 0)),
                  pl.BlockSpec((1,) + prm["pw"].shape[1:], lambda i: (i, 0, 0, 0, 0))],
        out_specs=pl.BlockSpec((bsz, chunk_len, SSM_CPG, nc), lambda i: (0, 0, i, 0)),
        out_shape=jax.ShapeDtypeStruct((bsz, chunk_len, D_A, nc), BF16),
        scratch_shapes=[pltpu.VMEM((lc, lc), BF16)],
        compiler_params=_cparams("arbitrary"),
        name="ssm",
    )(ut, xc, prm["kcat"], prm["wst"], prm["vst"], prm["pw"])


def _ssm_params(lam_re, lam_im, log_dt, b_re, b_im, c_re, c_im, d_skip, chunk_len, nc):
    hi = lax.Precision.HIGHEST
    L = chunk_len
    lam = lax.complex(lam_re.astype(F32), lam_im.astype(F32))
    dt = jnp.exp(log_dt.astype(F32))[..., None]
    lam_dt = lam * dt
    lam_bar = jnp.exp(lam_dt)
    b_bar = ((lam_bar - 1.0) / lam)[..., None] * lax.complex(b_re.astype(F32), b_im.astype(F32))
    c = lax.complex(c_re.astype(F32), c_im.astype(F32))
    j = jnp.arange(L + 1, dtype=F32)
    pw = jnp.exp(lam_dt[..., None] * j)
    kl = jnp.real(jnp.einsum('dgcp,dgpj,dgpi->dgjci', c, pw[..., :L], b_bar, precision=hi))
    d_g = d_skip.astype(F32).reshape(SSM_GROUPS, SSM_CPG)
    k0 = kl[0, :, 0] + kl[1, :, 0] + jax.vmap(jnp.diag)(d_g)
    kfull = jnp.concatenate([kl[0, :, :0:-1], k0[:, None], kl[1, :, 1:]], axis=1)
    kcat = jnp.transpose(kfull, (0, 2, 1, 3)).reshape(SSM_GROUPS, SSM_CPG, (2 * L - 1) * SSM_CPG)
    kcat = jnp.pad(kcat, ((0, 0), (0, 0), (0, SSM_CPG)))
    wf = pw[0][:, :, :L][:, :, ::-1, None] * b_bar[0][:, :, None, :]
    wb = pw[1][:, :, :L, None] * b_bar[1][:, :, None, :]
    lc = L * SSM_CPG
    wst = jnp.concatenate([jnp.real(wf), jnp.imag(wf), jnp.real(wb), jnp.imag(wb)], axis=1)
    wst = wst.reshape(SSM_GROUPS, 4 * SSM_STATE, lc)
    vf = c[0][:, None, :, :] * jnp.transpose(pw[0][:, :, 1:L + 1], (0, 2, 1))[:, :, None, :]
    vb = c[1][:, None, :, :] * jnp.transpose(pw[1][:, :, L:0:-1], (0, 2, 1))[:, :, None, :]
    vst = jnp.concatenate([jnp.real(vf), -jnp.imag(vf), jnp.real(vb), -jnp.imag(vb)], axis=-1)
    vst = vst.reshape(SSM_GROUPS, lc, 4 * SSM_STATE)
    n_steps = int(math.log2(nc))
    e = (L * (2.0 ** jnp.arange(n_steps, dtype=F32)))
    pl2 = jnp.exp(lam_dt[..., None] * e)
    pwk = jnp.stack([jnp.real(pl2[0]), jnp.imag(pl2[0]), jnp.real(pl2[1]), jnp.imag(pl2[1])], axis=1)
    pwk = jnp.transpose(pwk, (0, 3, 1, 2))[..., None]
    return {"kcat": kcat, "wst": wst.astype(BF16), "vst": vst.astype(BF16), "pw": pwk}


def _attn_kernel(q_ref, kp_ref, kc_ref, kn_ref, vp_ref, vc_ref, vn_ref, kx_ref, vx_ref, bias_ref, o_ref, *, tq):
    lane = lax.broadcasted_iota(jnp.int32, (1, LANES), 1)
    low = lane < NA_HEAD_DIM
    for hp in range(NA_HEADS // 2):
        sl = slice(hp * LANES, (hp + 1) * LANES)
        qp = q_ref[0, :, sl]
        ks = [kp_ref[0, :, sl], kc_ref[0, :, sl], kn_ref[0, :, sl], kx_ref[0, :, sl]]
        vs = [vp_ref[0, :, sl], vc_ref[0, :, sl], vn_ref[0, :, sl], vx_ref[0, :, sl]]
        outs = []
        for hh in range(2):
            h = 2 * hp + hh
            qm = jnp.where(low if hh == 0 else jnp.logical_not(low), qp, jnp.zeros_like(qp))
            s = [_dot_nt(qm, kk) for kk in ks]
            for j in range(3):
                s[j] = s[j] + bias_ref[0, h, :, j * tq:(j + 1) * tq].astype(F32)
            mx = s[0].max(axis=-1, keepdims=True)
            for sj in s[1:]:
                mx = jnp.maximum(mx, sj.max(axis=-1, keepdims=True))
            ps = [jnp.exp(sj - mx) for sj in s]
            den = ps[0].sum(axis=-1, keepdims=True)
            for pj in ps[1:]:
                den = den + pj.sum(axis=-1, keepdims=True)
            o = _dot(ps[0].astype(BF16), vs[0])
            for pj, vj in zip(ps[1:], vs[1:]):
                o = o + _dot(pj.astype(BF16), vj)
            outs.append(o / den)
        o_ref[0, :, sl] = jnp.where(low, outs[0], outs[1]).astype(BF16)


def _attn_bias_table(rpb, rows):
    r4 = ROWS_PER_QBLOCK
    nblk = rows // r4
    tq = r4 * GRID_W
    qr = np.repeat(np.arange(r4), GRID_W)
    qc = np.tile(np.arange(GRID_W), r4)
    kr_off = np.repeat(np.arange(-r4, 2 * r4), GRID_W)
    kc = np.tile(np.arange(GRID_W), 3 * r4)
    tables = []
    for blk in (0, min(1, nblk - 1), nblk - 1):
        r = blk * r4 + qr
        kr = blk * r4 + kr_off
        ws = np.clip(r - WIN_R // 2, 0, rows - WIN_R)
        cs = np.clip(qc - WIN_C // 2, 0, GRID_W - WIN_C)
        valid = ((kr[None, :] >= 0) & (kr[None, :] < rows)
                 & (kr[None, :] >= ws[:, None]) & (kr[None, :] < ws[:, None] + WIN_R)
                 & (kc[None, :] >= cs[:, None]) & (kc[None, :] < cs[:, None] + WIN_C))
        d_row = np.clip(kr[None, :] - r[:, None] + WIN_R - 1, 0, 2 * WIN_R - 2)
        d_col = np.clip(kc[None, :] - qc[:, None] + WIN_C - 1, 0, 2 * WIN_C - 2)
        bias = rpb.astype(F32)[:, d_row, d_col]
        tables.append(jnp.where(valid[None], bias, NEG_INF))
    return jnp.stack(tables).astype(BF16)


def _attn(q, k, v, kx, vx, bias):
    bsz, n, _ = q.shape
    tq = ROWS_PER_QBLOCK * GRID_W
    nblk = n // tq
    nctx = kx.shape[1]
    cur = lambda b, i: (b, i, 0)
    prev = lambda b, i: (b, jnp.maximum(i - 1, 0), 0)
    nxt = lambda b, i: (b, jnp.minimum(i + 1, nblk - 1), 0)
    var = lambda b, i: (jnp.where(i == 0, 0, jnp.where(i == nblk - 1, 2, 1)), 0, 0, 0)
    tok = lambda m: pl.BlockSpec((1, tq, D_B), m)
    ctx = pl.BlockSpec((1, nctx, D_B), lambda b, i: (b, 0, 0))
    return pl.pallas_call(
        functools.partial(_attn_kernel, tq=tq),
        grid=(bsz, nblk),
        in_specs=[tok(cur), tok(prev), tok(cur), tok(nxt), tok(prev), tok(cur), tok(nxt), ctx, ctx,
                  pl.BlockSpec((1, NA_HEADS, tq, 3 * tq), var)],
        out_specs=tok(cur),
        out_shape=jax.ShapeDtypeStruct((bsz, n, D_B), BF16),
        compiler_params=_cparams("arbitrary", "arbitrary"),
        name="attn",
    )(q, k, k, k, v, v, v, kx, vx, bias)


def _merge_kernel(y_ref, yb_ref, ga_ref, gb_ref, x_ref, g1_ref, sc_ref, sh_ref, gn_ref,
                  wglut_ref, bglu_ref, wpa_ref, wpb_ref, wout_ref, wr_ref,
                  x1_ref, h2_ref, aff_ref, *, slabs, nc):
    d = D_MODEL
    yt = jnp.concatenate([y_ref[0, s] for s in range(slabs)], axis=1).astype(F32)
    a = jax.nn.gelu(yt)
    z = _dot(wglut_ref[...], a.astype(BF16)) + bglu_ref[...]
    ya = (a * jax.nn.sigmoid(z)).T.astype(BF16)
    cat = lambda ref, w: jnp.concatenate([ref[0, :, s * w:(s + 1) * w] for s in range(slabs)], axis=0)
    pa = _dot(ya, wpa_ref[...])
    pb = _dot(cat(yb_ref, D_B), wpb_ref[...])
    m = cat(ga_ref, d).astype(F32) * pa + cat(gb_ref, d).astype(F32) * pb
    y = _dot(m.astype(BF16), wout_ref[...])
    x1 = cat(x_ref, d) + g1_ref[0] * y
    x1_ref[0] = x1
    h2 = _norm_mod(x1, gn_ref[...], sc_ref[0], sh_ref[0]).astype(BF16)
    h2_ref[0] = h2
    logits = _dot(h2, wr_ref[...])
    e = jnp.exp(logits - logits.max(axis=-1, keepdims=True))
    aff_ref[0] = e / e.sum(axis=-1, keepdims=True)


def _merge(y, yb, ga, gb, x, g1, scale, shift, gain, w, chunk_len, slabs=4):
    bsz, n, d = x.shape
    nc = n // chunk_len
    tb = slabs * nc
    const = lambda shape: pl.BlockSpec(shape, lambda b, j: (0,) * len(shape))
    view = lambda width: pl.BlockSpec((1, nc, slabs * width), lambda b, j: (b, 0, j))
    vec = pl.BlockSpec((1, 1, d), lambda b, j: (b, 0, 0))
    return pl.pallas_call(
        functools.partial(_merge_kernel, slabs=slabs, nc=nc),
        grid=(bsz, chunk_len // slabs),
        in_specs=[pl.BlockSpec((1, slabs, D_A, nc), lambda b, j: (b, j, 0, 0)),
                  view(D_B), view(d), view(d), view(d), vec, vec, vec, const((1, d)),
                  const((D_A, D_A)), const((D_A, 1)), const((D_A, d)), const((D_B, d)), const((d, d)),
                  const((d, N_EXPERTS))],
        out_specs=[pl.BlockSpec((1, tb, d), lambda b, j: (b, j, 0)),
                   pl.BlockSpec((1, tb, d), lambda b, j: (b, j, 0)),
                   pl.BlockSpec((1, tb, N_EXPERTS), lambda b, j: (b, j, 0))],
        out_shape=[jax.ShapeDtypeStruct((bsz, n, d), F32),
                   jax.ShapeDtypeStruct((bsz, n, d), BF16),
                   jax.ShapeDtypeStruct((bsz, n, N_EXPERTS), F32)],
        compiler_params=_cparams("arbitrary", "arbitrary"),
        name="merge",
    )(y, yb.reshape(bsz, nc, chunk_len * D_B), ga, gb, x.reshape(bsz, nc, chunk_len * d), g1, scale, shift, gain,
      w["glut"], w["bglu"], w["pa"], w["pb"], w["out"], w["router"])


def _route_kernel(aff_ref, tri_ref, eye_ref, pos_ref, post_ref, meta_ref, cend_ref, cum_scr,
                  *, cap, chunk_len, nc, tb, st, tbc):
    n = aff_ref.shape[1]
    ne = N_EXPERTS
    bits = pltpu.bitcast(aff_ref[0], jnp.int32)
    count = lambda m: jnp.sum(m.astype(F32), axis=0, keepdims=True)

    def thr_step(_, c):
        lo, hi = c
        mid = lo + ((hi - lo + 1) >> 1)
        ok = count(bits >= mid) >= cap
        return jnp.where(ok, mid, lo), jnp.where(ok, hi, mid - 1)

    lo0 = jnp.zeros((1, ne), jnp.int32)
    thr, _ = lax.fori_loop(0, 31, thr_step, (lo0, jnp.full((1, ne), 0x7F800000, jnp.int32)))
    gt = bits > thr
    eq = bits == thr
    need = cap - count(gt)
    p = lax.broadcasted_iota(jnp.int32, (n, 1), 0)
    tok = (p & (nc - 1)) * chunk_len + (p >> int(math.log2(nc)))

    def tie_step(_, c):
        lo, hi = c
        mid = (lo + hi) >> 1
        ok = count(jnp.logical_and(eq, tok < mid)) >= need
        return jnp.where(ok, lo, mid + 1), jnp.where(ok, mid, hi)

    _, cut = lax.fori_loop(0, int(math.log2(n)) + 1, tie_step, (lo0, jnp.full((1, ne), n, jnp.int32)))
    sel = jnp.logical_or(gt, jnp.logical_and(eq, tok < cut))
    selb = sel.astype(BF16)
    carry = jnp.zeros((1, ne), F32)
    for i in range(n // tb):
        blk = slice(i * tb, (i + 1) * tb)
        cs = _dot(tri_ref[...], selb[blk]) + carry
        carry = cs[tb - 1:tb]
        cum_scr[blk, :] = cs
        slot1 = jnp.where(sel[blk], cs, 0.0)
        pos_ref[0, blk, :] = slot1
        hi_part = jnp.floor(slot1 * (1.0 / 32.0))
        lo_part = slot1 - 32.0 * hi_part
        t = 32.0 * _dot_nt(eye_ref[...], hi_part.astype(BF16)) + _dot_nt(eye_ref[...], lo_part.astype(BF16))
        for e in range(ne):
            post_ref[0, e, i:i + 1, :] = t[e:e + 1, :]
    cum = cum_scr[...]
    rows = []
    for s in range(cap // st):
        rows.append(count(cum <= float(st * s)))
    for s in range(cap // st):
        rows.append(count(cum <= float(st * (s + 1) - 1)))
    meta_ref[0] = jnp.concatenate(rows, axis=0)
    cend_ref[0] = jnp.concatenate([cum[(i + 1) * tbc - 1:(i + 1) * tbc] for i in range(n // tbc)], axis=0)


def _route(aff, cap, chunk_len, tb, st, tbc):
    bsz, n, ne = aff.shape
    nc = n // chunk_len
    nst = cap // st
    tri = jnp.asarray(np.tril(np.ones((tb, tb), np.float32)), BF16)
    eye = jnp.asarray(np.eye(ne, dtype=np.float32), BF16)
    return pl.pallas_call(
        functools.partial(_route_kernel, cap=cap, chunk_len=chunk_len, nc=nc, tb=tb, st=st, tbc=tbc),
        grid=(bsz,),
        in_specs=[pl.BlockSpec((1, n, ne), lambda b: (b, 0, 0)),
                  pl.BlockSpec((tb, tb), lambda b: (0, 0)),
                  pl.BlockSpec((ne, ne), lambda b: (0, 0))],
        out_specs=[pl.BlockSpec((1, n, ne), lambda b: (b, 0, 0)),
                   pl.BlockSpec((1, ne, n // tb, tb), lambda b: (b, 0, 0, 0)),
                   pl.BlockSpec((1, 2 * nst, ne), lambda b: (b, 0, 0)),
                   pl.BlockSpec((1, n // tbc, ne), lambda b: (b, 0, 0))],
        out_shape=[jax.ShapeDtypeStruct((bsz, n, ne), F32),
                   jax.ShapeDtypeStruct((bsz, ne, n // tb, tb), F32),
                   jax.ShapeDtypeStruct((bsz, 2 * nst, ne), F32),
                   jax.ShapeDtypeStruct((bsz, n // tbc, ne), F32)],
        scratch_shapes=[pltpu.VMEM((n, ne), F32)],
        compiler_params=_cparams("arbitrary"),
        name="route",
    )(aff, tri, eye)


def _gather_kernel(lo_ref, hi_ref, post_ref, h_ref, xe_ref, acc, *, st, tb, nst):
    b, e, s = pl.program_id(0), pl.program_id(1), pl.program_id(2)
    idx = (b * N_EXPERTS + e) * nst + s
    slot1 = (s * st + 1 + lax.broadcasted_iota(jnp.int32, (st, 1), 0)).astype(F32)
    acc[...] = jnp.zeros_like(acc)

    def body(blk, _):
        onehot = (post_ref[0, 0, pl.ds(blk, 1), :] == slot1).astype(BF16)
        acc[...] += _dot(onehot, h_ref[0, pl.ds(pl.multiple_of(blk * tb, tb), tb), :])
        return 0

    lax.fori_loop(lo_ref[idx], hi_ref[idx] + 1, body, 0)
    xe_ref[0, 0] = acc[...].astype(BF16)


def _gather(post, h2, blk_lo, blk_hi, cap, st, tb):
    bsz, n, d = h2.shape
    nst = cap // st
    return pl.pallas_call(
        functools.partial(_gather_kernel, st=st, tb=tb, nst=nst),
        grid_spec=pltpu.PrefetchScalarGridSpec(
            num_scalar_prefetch=2,
            grid=(bsz, N_EXPERTS, nst),
            in_specs=[pl.BlockSpec((1, 1, n // tb, tb), lambda b, e, s, lo, hi: (b, e, 0, 0)),
                      pl.BlockSpec((1, n, d), lambda b, e, s, lo, hi: (b, 0, 0))],
            out_specs=pl.BlockSpec((1, 1, st, d), lambda b, e, s, lo, hi: (e, b, s, 0)),
            scratch_shapes=[pltpu.VMEM((st, d), F32)]),
        out_shape=jax.ShapeDtypeStruct((N_EXPERTS, bsz, cap, d), BF16),
        compiler_params=_cparams("arbitrary", "arbitrary", "arbitrary"),
        name="gather",
    )(blk_lo, blk_hi, post, h2)


def _ffn_kernel(xe_ref, wg_ref, wu_ref, wd_ref, ye_ref, acc):
    f = pl.program_id(1)
    x = xe_ref[0]
    g = _dot(x, wg_ref[0].astype(BF16))
    u = _dot(x, wu_ref[0].astype(BF16))
    hid = (g * jax.nn.sigmoid(g) * u).astype(BF16)
    part = _dot(hid, wd_ref[0].astype(BF16))

    @pl.when(f == 0)
    def _():
        acc[...] = part

    @pl.when(f > 0)
    def _():
        acc[...] += part

    @pl.when(f == pl.num_programs(1) - 1)
    def _():
        ye_ref[0] = acc[...].astype(BF16)


def _ffn(xe, w_gate, w_up, w_down, fb=256):
    ne, m, d = xe.shape
    dff = w_gate.shape[-1]
    return pl.pallas_call(
        _ffn_kernel,
        grid=(ne, dff // fb),
        in_specs=[pl.BlockSpec((1, m, d), lambda e, f: (e, 0, 0)),
                  pl.BlockSpec((1, d, fb), lambda e, f: (e, 0, f)),
                  pl.BlockSpec((1, d, fb), lambda e, f: (e, 0, f)),
                  pl.BlockSpec((1, fb, d), lambda e, f: (e, f, 0))],
        out_specs=pl.BlockSpec((1, m, d), lambda e, f: (e, 0, 0)),
        out_shape=jax.ShapeDtypeStruct((ne, m, d), BF16),
        scratch_shapes=[pltpu.VMEM((m, d), F32)],
        compiler_params=_cparams("arbitrary", "arbitrary"),
        name="ffn",
    )(xe, w_gate, w_up, w_down)


def _combine_partial_kernel(lo_ref, hi_ref, pos_ref, aff_ref, ye_ref, o_ref, acc, *, e0, n_e, ch, nblk):
    b, i = pl.program_id(0), pl.program_id(1)
    _combine_accumulate(lo_ref, hi_ref, pos_ref, aff_ref, ye_ref, acc, b, i, e0, n_e, ch, nblk)
    o_ref[0] = acc[...]


def _combine_final_kernel(lo_ref, hi_ref, pos_ref, aff_ref, ye_ref, part_ref, x1_ref, g2_ref, o_ref, acc,
                          *, e0, n_e, ch, nblk, nc):
    b, i = pl.program_id(0), pl.program_id(1)
    d = D_MODEL
    _combine_accumulate(lo_ref, hi_ref, pos_ref, aff_ref, ye_ref, acc, b, i, e0, n_e, ch, nblk)
    res = x1_ref[0] + g2_ref[0] * (part_ref[0] + acc[...])
    for s in range(res.shape[0] // nc):
        o_ref[0, :, s * d:(s + 1) * d] = res[s * nc:(s + 1) * nc]


def _combine_accumulate(lo_ref, hi_ref, pos_ref, aff_ref, ye_ref, acc, b, i, e0, n_e, ch, nblk):
    lane1 = (1 + lax.broadcasted_iota(jnp.int32, (1, ch), 1)).astype(F32)
    acc[...] = jnp.zeros_like(acc)
    for k in range(n_e):
        e = e0 + k
        pcol = pos_ref[0, :, e:e + 1]
        acol = aff_ref[0, :, e:e + 1]
        idx = (b * nblk + i) * N_EXPERTS + e

        def body(j, _, k=k, pcol=pcol, acol=acol):
            onehot = (pcol - (j * ch).astype(F32) == lane1).astype(BF16)
            z = _dot(onehot, ye_ref[k, 0, pl.ds(pl.multiple_of(j * ch, ch), ch), :])
            acc[...] += acol * z
            return 0

        lax.fori_loop(lo_ref[idx], hi_ref[idx] + 1, body, 0)


def _combine(pos, aff, ye, ch_lo, ch_hi, x1, g2, chunk_len, tbc, ch):
    bsz, n, d = x1.shape
    ne, _, cap, _ = ye.shape
    nc = n // chunk_len
    nblk = n // tbc
    half = ne // 2
    tokb = lambda w: pl.BlockSpec((1, tbc, w), lambda b, i, lo, hi: (b, i, 0))
    yspec = lambda h: pl.BlockSpec((half, 1, cap, d), lambda b, i, lo, hi: (h, b, 0, 0))
    part = pl.pallas_call(
        functools.partial(_combine_partial_kernel, e0=0, n_e=half, ch=ch, nblk=nblk),
        grid_spec=pltpu.PrefetchScalarGridSpec(
            num_scalar_prefetch=2, grid=(bsz, nblk),
            in_specs=[tokb(ne), tokb(ne), yspec(0)],
            out_specs=tokb(d),
            scratch_shapes=[pltpu.VMEM((tbc, d), F32)]),
        out_shape=jax.ShapeDtypeStruct((bsz, n, d), F32),
        compiler_params=_cparams("arbitrary", "arbitrary"),
        name="combine_a",
    )(ch_lo, ch_hi, pos, aff, ye)
    slabs = tbc // nc
    out = pl.pallas_call(
        functools.partial(_combine_final_kernel, e0=half, n_e=half, ch=ch, nblk=nblk, nc=nc),
        grid_spec=pltpu.PrefetchScalarGridSpec(
            num_scalar_prefetch=2, grid=(bsz, nblk),
            in_specs=[tokb(ne), tokb(ne), yspec(1), tokb(d), tokb(d),
                      pl.BlockSpec((1, 1, d), lambda b, i, lo, hi: (b, 0, 0))],
            out_specs=pl.BlockSpec((1, nc, slabs * d), lambda b, i, lo, hi: (b, 0, i)),
            scratch_shapes=[pltpu.VMEM((tbc, d), F32)]),
        out_shape=jax.ShapeDtypeStruct((bsz, nc, chunk_len * d), F32),
        compiler_params=_cparams("arbitrary", "arbitrary"),
        name="combine_b",
    )(ch_lo, ch_hi, pos, aff, ye, part, x1, g2)
    return out.reshape(bsz, n, d)


def _moe(h2, aff, x1, g2, w_gate, w_up, w_down, chunk_len):
    bsz, n, d = h2.shape
    ne = N_EXPERTS
    cap = CAPACITY_FACTOR * n // ne
    tb = 256
    st = min(256, cap)
    tbc = 4 * (n // chunk_len)
    ch = min(128, cap)
    pos, post, meta, cend = _route(aff, cap, chunk_len, tb, st, tbc)
    nst = cap // st
    first = meta[:, :nst, :].astype(jnp.int32)
    last = meta[:, nst:, :].astype(jnp.int32)
    blk_lo = jnp.transpose(first // tb, (0, 2, 1)).reshape(-1)
    blk_hi = jnp.transpose(jnp.minimum(last // tb, n // tb - 1), (0, 2, 1)).reshape(-1)
    xe = _gather(post, h2, blk_lo, blk_hi, cap, st, tb)
    ye = _ffn(xe.reshape(ne, bsz * cap, d), w_gate, w_up, w_down).reshape(ne, bsz, cap, d)
    c1 = cend.astype(jnp.int32)
    c0 = jnp.concatenate([jnp.zeros_like(c1[:, :1]), c1[:, :-1]], axis=1)
    empty = c1 == c0
    ch_lo = jnp.where(empty, 1, c0 // ch).reshape(-1)
    ch_hi = jnp.where(empty, 0, (c1 - 1) // ch).reshape(-1)
    return _combine(pos, aff, ye, ch_lo, ch_hi, x1, g2, chunk_len, tbc, ch)


def kernel(x, c, ctx, c_ctx, w_ada, b_ada, norm_mix, norm_ffn, w_in, ssm_lam_re, ssm_lam_im, ssm_log_dt,
           ssm_b_re, ssm_b_im, ssm_c_re, ssm_c_im, ssm_d, w_glu, b_glu, q_norm, k_norm, na_rpb, w_proj_a,
           w_proj_b, w_out, w_router, w_e_gate, w_e_up, w_e_down):
    bsz, n, d = x.shape
    n_ctx = ctx.shape[1]
    assert d == D_MODEL and w_ada.shape[0] == 1 and n % (NCHUNK * 8) == 0 and n % (ROWS_PER_QBLOCK * GRID_W) == 0
    chunk_len = n // NCHUNK
    assert n_ctx % chunk_len == 0 and bsz * (n_ctx // chunk_len) <= LANES and n_ctx % LANES == 0
    nctx_chunks = n_ctx // chunk_len

    cc = jnp.zeros((8, d), F32).at[:bsz].set(c).at[bsz].set(c_ctx)
    mod = _adaln(cc, w_ada[0], b_ada[0])
    sh1, sc1, g1, sh2, sc2, g2 = [m[:bsz].reshape(bsz, 1, d) for m in jnp.split(mod, 6, axis=-1)]
    csh1, csc1 = mod[bsz:bsz + 1, 0:d], mod[bsz:bsz + 1, d:2 * d]

    w = w_in[0].astype(BF16)
    o = np.cumsum([0, D_A, D_B, D_B, D_B, d, d])
    wsplit = {"u": w[:, o[0]:o[1]], "ut": w[:, o[0]:o[1]].T, "q": w[:, o[1]:o[2]], "k": w[:, o[2]:o[3]],
              "v": w[:, o[3]:o[4]], "ga": w[:, o[4]:o[5]], "gb": w[:, o[5]:o[6]]}
    hsum = jnp.asarray(np.kron(np.eye(NA_HEADS), np.full((NA_HEAD_DIM, NA_HEAD_DIM), 1.0 / NA_HEAD_DIM)), BF16)
    qn = (jnp.tile(q_norm[0].astype(F32), NA_HEADS) * (NA_HEAD_DIM ** -0.5)).reshape(1, D_B)
    kn = jnp.tile(k_norm[0].astype(F32), NA_HEADS).reshape(1, D_B)
    gain1 = norm_mix[0].astype(F32).reshape(1, d)

    ut, q, k, v, ga, gb = _inproj(x, sc1, sh1, gain1, wsplit, hsum, qn, kn, chunk_len)
    u_ctx, kx, vx = _inproj_ctx(ctx.reshape(bsz * n_ctx, d), csc1, csh1, gain1, wsplit, hsum, kn)

    xc = u_ctx.reshape(bsz, nctx_chunks, chunk_len, D_A).transpose(2, 3, 0, 1).reshape(chunk_len, D_A, -1)
    xc = jnp.pad(xc, ((0, 0), (0, 0), (0, LANES - xc.shape[-1]))).astype(BF16)
    prm = _ssm_params(ssm_lam_re[0], ssm_lam_im[0], ssm_log_dt[0], ssm_b_re[0], ssm_b_im[0], ssm_c_re[0],
                      ssm_c_im[0], ssm_d[0], chunk_len, NCHUNK)
    y_ssm = _ssm(ut, xc, prm, chunk_len, nctx_chunks)

    bias = _attn_bias_table(na_rpb[0], n // GRID_W)
    yb = _attn(q, k, v, kx.reshape(bsz, n_ctx, D_B), vx.reshape(bsz, n_ctx, D_B), bias)

    wm = {"glut": w_glu[0].T.astype(BF16), "bglu": b_glu[0].astype(F32).reshape(D_A, 1),
          "pa": w_proj_a[0].astype(BF16), "pb": w_proj_b[0].astype(BF16), "out": w_out[0].astype(BF16),
          "router": w_router[0].astype(BF16)}
    x1, h2, aff = _merge(y_ssm, yb, ga, gb, x, g1, sc2, sh2, norm_ffn[0].astype(F32).reshape(1, d), wm, chunk_len)

    return _moe(h2, aff, x1, g2, w_e_gate[0], w_e_up[0], w_e_down[0], chunk_len)
```
